```python
import math
import jax, jax.numpy as jnp
from jax import lax
import numpy as np

D_MODEL = 4096
BATCH = 1
SEQ = 16384
DEPTH = 4

MLA_HEADS = 6
QK_NOPE = 128
QK_ROPE = 64
QK_HEAD = QK_NOPE + QK_ROPE
V_HEAD = 128
Q_LORA = 768
KV_LORA = 512
ROPE_THETA = 10000.0
Q_BLK = 128

DIL_CONFIGS = ((128, 1), (512, 4), (2048, 16))
DIL_GROUPS = 3
DIL_SLOTS = 6
DIL_QK = 64
DIL_V = 128
DIL_BLK = 128

POOL_WINDOWS = (2, 4, 8, 16)
POOL_GROUPS = 4
POOL_GROUP_CH = 128
POOL_CH = POOL_GROUPS * POOL_GROUP_CH

N_BUCKETS = 32
MAX_DISTANCE = 2048

N_BRANCH = 3
MLA_OUT = MLA_HEADS * V_HEAD
DIL_OUT = DIL_SLOTS * DIL_V
MIX_W = MLA_OUT + DIL_OUT + POOL_CH
D_FF = -(-8 * D_MODEL // (3 * 256)) * 256
EPS = 1e-6

IN_SIZES = (Q_LORA, KV_LORA, QK_ROPE, DIL_GROUPS * DIL_SLOTS * DIL_QK, DIL_SLOTS * DIL_QK,
            DIL_SLOTS * DIL_V, POOL_CH, N_BRANCH * D_MODEL)
IN_COLS = sum(IN_SIZES)
IN_OFFSETS = tuple(int(v) for v in np.cumsum(IN_SIZES)[:-1])

kernel_name = 'hybrid_mla_dilated_pool_gated'


def rms_norm(x, g):
    xf = x.astype(jnp.float32)
    y = xf * lax.rsqrt(jnp.mean(xf * xf, axis=-1, keepdims=True) + EPS)
    return (y * g.astype(jnp.float32)).astype(x.dtype)


def rope(x, pos):
    half = x.shape[-1] // 2
    inv = ROPE_THETA ** (-jnp.arange(half, dtype=jnp.float32) / half)
    ang = pos.astype(jnp.float32)[:, None] * inv[None, :]
    cos = jnp.cos(ang)[None, :, None, :]
    sin = jnp.sin(ang)[None, :, None, :]
    xf = x.astype(jnp.float32)
    x1, x2 = xf[..., :half], xf[..., half:]
    return jnp.concatenate([x1 * cos - x2 * sin, x1 * sin + x2 * cos], axis=-1).astype(x.dtype)


def t5_bucket(dist):
    max_exact = N_BUCKETS // 2
    d = jnp.maximum(dist, 1).astype(jnp.float32)
    large = max_exact + (jnp.log(d / max_exact) / math.log(MAX_DISTANCE / max_exact)
                         * (N_BUCKETS - max_exact)).astype(jnp.int32)
    large = jnp.minimum(large, N_BUCKETS - 1)
    return jnp.where(dist < max_exact, dist, large)


def causal_attention(q, k, v, scale):
    B, S, H, dk = q.shape
    dv = v.shape[-1]
    nb = S // Q_BLK
    qb = q.reshape(B, nb, Q_BLK, H, dk).transpose(1, 0, 2, 3, 4)
    kpos = jnp.arange(S)

    def one_block(args):
        q_i, i = args
        s = jnp.einsum('bqhd,bkhd->bhqk', q_i, k, preferred_element_type=jnp.float32) * scale
        qpos = i * Q_BLK + jnp.arange(Q_BLK)
        mask = kpos[None, :] <= qpos[:, None]
        p = jax.nn.softmax(jnp.where(mask[None, None], s, -jnp.inf), axis=-1)
        return jnp.einsum('bhqk,bkhd->bqhd', p.astype(v.dtype), v)

    out = lax.map(one_block, (qb, jnp.arange(nb)))
    return out.transpose(1, 0, 2, 3, 4).reshape(B, S, H, dv)


def mla_branch(h_q, h_kv, h_kr, q_a_norm, kv_a_norm, w_uq, w_ukv, q_norm, k_norm, pos):
    B, S, _ = h_q.shape
    c_q = rms_norm(h_q, q_a_norm)
    q = (c_q @ w_uq).reshape(B, S, MLA_HEADS, QK_HEAD)
    c_kv = rms_norm(h_kv, kv_a_norm)
    kv = (c_kv @ w_ukv).reshape(B, S, MLA_HEADS, QK_NOPE + V_HEAD)
    k_nope, v = kv[..., :QK_NOPE], kv[..., QK_NOPE:]
    k_pe = jnp.broadcast_to(h_kr[:, :, None, :], (B, S, MLA_HEADS, QK_ROPE))
    k = jnp.concatenate([k_nope, k_pe], axis=-1)
    q = rms_norm(q, q_norm)
    k = rms_norm(k, k_norm)
    q = jnp.concatenate([q[..., :QK_NOPE], rope(q[..., QK_NOPE:], pos)], axis=-1)
    k = jnp.concatenate([k[..., :QK_NOPE], rope(k[..., QK_NOPE:], pos)], axis=-1)
    o = causal_attention(q, k, v, QK_HEAD ** -0.5)
    return o.reshape(B, S, MLA_OUT)


def dilated_group(q, k, v, window, r, bias_tab):
    B, S, J, dk = q.shape
    span = window // r
    unit = r * DIL_BLK
    s_pad = -(-S // unit) * unit
    L = s_pad // r
    nb = L // DIL_BLK

    def strided(t):
        d = t.shape[-1]
        t = jnp.pad(t, ((0, 0), (0, s_pad - S), (0, 0), (0, 0)))
        t = t.reshape(B, L, r, J, d).transpose(0, 2, 1, 3, 4)
        return t.reshape(B, r, nb, DIL_BLK, J, d)

    def with_prev(t):
        prev = jnp.concatenate([jnp.zeros_like(t[:, :, :1]), t[:, :, :-1]], axis=2)
        return jnp.concatenate([prev, t], axis=3)

    qs = strided(q)
    kk = with_prev(strided(k))
    vv = with_prev(strided(v))
    s = jnp.einsum('brnqjd,brnkjd->brnjqk', qs, kk, preferred_element_type=jnp.float32) * (dk ** -0.5)
    qi = jnp.arange(DIL_BLK)[:, None]
    ki = jnp.arange(2 * DIL_BLK)[None, :]
    dist = DIL_BLK + qi - ki
    band = (dist >= 0) & (dist <= span)
    bias = bias_tab[t5_bucket(jnp.clip(dist, 0, span) * r)].astype(jnp.float32)
    bias = bias.transpose(2, 0, 1)
    blk = jnp.arange(nb)[:, None, None]
    valid = band[None] & ((blk > 0) | (ki[None] >= DIL_BLK))
    logits = jnp.where(valid[None, None, :, None], s + bias[None, None, None], -jnp.inf)
    m = jnp.max(logits, axis=-1)
    p = jnp.exp(logits - m[..., None])
    l = jnp.sum(p, axis=-1)
    o = jnp.einsum('brnjqk,brnkjd->brnqjd', p.astype(v.dtype), vv, preferred_element_type=jnp.float32)

    def unstride(t):
        t = t.reshape((B, r, L) + t.shape[4:])
        t = jnp.swapaxes(t, 1, 2)
        return t.reshape((B, s_pad) + t.shape[3:])[:, :S]

    return (unstride(o), unstride(jnp.swapaxes(m, 3, 4)), unstride(jnp.swapaxes(l, 3, 4)))


def dilated_branch(h_q, h_k, h_v, q_norm, k_norm, rel_bias):
    B, S, _ = h_q.shape
    q = rms_norm(h_q.reshape(B, S, DIL_GROUPS, DIL_SLOTS, DIL_QK), q_norm)
    k = rms_norm(h_k.reshape(B, S, DIL_SLOTS, DIL_QK), k_norm)
    v = h_v.reshape(B, S, DIL_SLOTS, DIL_V)
    outs, maxs, dens = [], [], []
    for g, (window, r) in enumerate(DIL_CONFIGS):
        o, m, l = dilated_group(q[:, :, g], k, v, window, r,
                                rel_bias[:, g * DIL_SLOTS:(g + 1) * DIL_SLOTS])
        outs.append(o)
        maxs.append(m)
        dens.append(l)
    m_all = jnp.stack(maxs)
    wts = jnp.exp(m_all - jnp.max(m_all, axis=0, keepdims=True))
    num = jnp.sum(wts[..., None] * jnp.stack(outs), axis=0)
    den = jnp.sum(wts * jnp.stack(dens), axis=0)
    y = (num / den[..., None]).astype(v.dtype)
    return y.reshape(B, S, DIL_OUT)


def pool_branch(p, pool_w, pool_scale):
    B, S, _ = p.shape
    pf = p.astype(jnp.float32).reshape(B, S, POOL_GROUPS, POOL_GROUP_CH)
    cs = jnp.concatenate([jnp.zeros((B, 1, POOL_GROUPS, POOL_GROUP_CH), jnp.float32),
                          jnp.cumsum(pf, axis=1)], axis=1)
    t = jnp.arange(S)[:, None]
    w = jnp.array(POOL_WINDOWS, dtype=jnp.int32)[None, :]
    lo = jnp.maximum(t + 1 - w, 0)
    gid = jnp.arange(POOL_GROUPS)[None, :]
    wsum = cs[:, 1:] - cs[:, lo, gid, :]
    cnt = jnp.minimum(t + 1, w).astype(jnp.float32)
    pooled = (wsum / cnt[None, :, :, None] - pf).astype(p.dtype)
    y = jnp.einsum('bsgc,gcd->bsgd', pooled, pool_w) * pool_scale.reshape(POOL_GROUPS, POOL_GROUP_CH)
    return y.reshape(B, S, POOL_CH)


def setup_inputs(seed: int = 0) -> dict:
    key = jax.random.key(seed)
    ks = jax.random.split(key, 19)

    def nrm(k, shape, scale):
        return jax.random.normal(k, shape, jnp.float32) * scale

    def gain(k, shape):
        return 1.0 + 0.1 * jax.random.normal(k, shape, jnp.float32)

    return {
        'x': nrm(ks[0], (BATCH, SEQ, D_MODEL), 1.0),
        'w_in': nrm(ks[1], (DEPTH, D_MODEL, IN_COLS), D_MODEL ** -0.5),
        'attn_norm': gain(ks[2], (DEPTH, D_MODEL)),
        'q_a_norm': gain(ks[3], (DEPTH, Q_LORA)),
        'kv_a_norm': gain(ks[4], (DEPTH, KV_LORA)),
        'w_uq': nrm(ks[5], (DEPTH, Q_LORA, MLA_HEADS * QK_HEAD), Q_LORA ** -0.5),
        'w_ukv': nrm(ks[6], (DEPTH, KV_LORA, MLA_HEADS * (QK_NOPE + V_HEAD)), KV_LORA ** -0.5),
        'mla_q_norm': gain(ks[7], (DEPTH, QK_HEAD)),
        'mla_k_norm': gain(ks[8], (DEPTH, QK_HEAD)),
        'dil_q_norm': gain(ks[9], (DEPTH, DIL_QK)),
        'dil_k_norm': gain(ks[10], (DEPTH, DIL_QK)),
        'rel_bias': nrm(ks[11], (N_BUCKETS, DIL_GROUPS * DIL_SLOTS), 0.5),
        'pool_w': nrm(ks[12], (DEPTH, POOL_GROUPS, POOL_GROUP_CH, POOL_GROUP_CH), POOL_GROUP_CH ** -0.5),
        'pool_scale': gain(ks[13], (DEPTH, POOL_CH)),
        'w_branch': nrm(ks[14], (DEPTH, MIX_W, D_MODEL), MLA_OUT ** -0.5),
        'w_out': nrm(ks[15], (DEPTH, D_MODEL, D_MODEL), D_MODEL ** -0.5),
        'ffn_norm': gain(ks[16], (DEPTH, D_MODEL)),
        'w_gate_up': nrm(ks[17], (DEPTH, D_MODEL, 2 * D_FF), D_MODEL ** -0.5),
        'w_down': nrm(ks[18], (DEPTH, D_FF, D_MODEL), D_FF ** -0.5),
    }


def reference(x, w_in, attn_norm, q_a_norm, kv_a_norm, w_uq, w_ukv, mla_q_norm, mla_k_norm,
              dil_q_norm, dil_k_norm, rel_bias, pool_w, pool_scale, w_branch, w_out,
              ffn_norm, w_gate_up, w_down):
    B, S, _ = x.shape
    pos = jnp.arange(S)
    for l in range(DEPTH):
        h = rms_norm(x, attn_norm[l])
        h_q, h_kv, h_kr, d_q, d_k, d_v, h_pool, h_gate = jnp.split(h @ w_in[l], IN_OFFSETS, axis=-1)
        y_mla = mla_branch(h_q, h_kv, h_kr, q_a_norm[l], kv_a_norm[l], w_uq[l], w_ukv[l],
                           mla_q_norm[l], mla_k_norm[l], pos)
        y_dil = dilated_branch(d_q, d_k, d_v, dil_q_norm[l], dil_k_norm[l], rel_bias)
        y_pool = pool_branch(h_pool, pool_w[l], pool_scale[l])
        wb = w_branch[l]
        z_mla = y_mla @ wb[:MLA_OUT]
        z_dil = y_dil @ wb[MLA_OUT:MLA_OUT + DIL_OUT]
        z_pool = y_pool @ wb[MLA_OUT + DIL_OUT:]
        gates = jax.nn.sigmoid(h_gate.reshape(B, S, N_BRANCH, D_MODEL))
        merged = gates[:, :, 0] * z_mla + gates[:, :, 1] * z_dil + gates[:, :, 2] * z_pool
        x = x + merged @ w_out[l]
        h2 = rms_norm(x, ffn_norm[l])
        a, b = jnp.split(h2 @ w_gate_up[l], 2, axis=-1)
        x = x + (jax.nn.silu(a) * b) @ w_down[l]
    return x
```

```python
import functools
import math

import jax
import jax.numpy as jnp
import numpy as np
from jax import lax
from jax.experimental import pallas as pl
from jax.experimental.pallas import tpu as pltpu

F32 = jnp.float32
BF16 = jnp.bfloat16

D_MODEL = 4096
MLA_HEADS = 6
QK_NOPE = 128
QK_ROPE = 64
QK_HEAD = QK_NOPE + QK_ROPE
V_HEAD = 128
Q_LORA = 768
KV_LORA = 512
ROPE_THETA = 10000.0
DIL_CONFIGS = ((128, 1), (512, 4), (2048, 16))
DIL_GROUPS = 3
DIL_SLOTS = 6
DIL_QK = 64
DIL_V = 128
DIL_BLK = 128
POOL_WINDOWS = (2, 4, 8, 16)
POOL_GROUPS = 4
POOL_GROUP_CH = 128
POOL_CH = POOL_GROUPS * POOL_GROUP_CH
N_BUCKETS = 32
MAX_DISTANCE = 2048
N_BRANCH = 3
MLA_OUT = MLA_HEADS * V_HEAD
DIL_OUT = DIL_SLOTS * DIL_V
MIX_W = MLA_OUT + DIL_OUT + POOL_CH
D_FF = -(-8 * D_MODEL // (3 * 256)) * 256
EPS = 1e-6
NEG = -1e30

_IN_SIZES = (Q_LORA, KV_LORA, QK_ROPE, DIL_GROUPS * DIL_SLOTS * DIL_QK, DIL_SLOTS * DIL_QK,
             DIL_SLOTS * DIL_V, POOL_CH, N_BRANCH * D_MODEL)
_OFF = tuple(int(v) for v in np.cumsum((0,) + _IN_SIZES))
O_HQ, O_HKV, O_HKR, O_DQ, O_DK, O_DV, O_POOL, O_GATE = _OFF[:8]

DQK_W = DIL_SLOTS * DIL_QK
PS_DQ = 0
PS_DK = 3 * DQK_W
PS_DV = PS_DK + DQK_W
PS_HQ = PS_DV + DIL_OUT
PS_HKV = PS_HQ + Q_LORA
PS_POOL = PS_HKV + KV_LORA
PS_KR = PS_POOL + POOL_CH
PS_USED = PS_KR + 256
PS_W = 4608

LANE = 128
QK_PAD = 256
D_FFP = 11264
VMEM_LIMIT = 56 * 1024 * 1024


def _cparams(n_axes, vmem=VMEM_LIMIT):
    return pltpu.CompilerParams(dimension_semantics=("arbitrary",) * n_axes,
                                vmem_limit_bytes=vmem)


def _rmsnorm_kernel(x_ref, g_ref, o_ref):
    x = x_ref[...]
    ms = jnp.mean(x * x, axis=-1, keepdims=True)
    o_ref[...] = (x * lax.rsqrt(ms + EPS) * g_ref[...]).astype(o_ref.dtype)


def rmsnorm_bf16(x, g, tm=256):
    m, d = x.shape
    tm = min(tm, m)
    return pl.pallas_call(
        _rmsnorm_kernel,
        grid=(m // tm,),
        in_specs=[pl.BlockSpec((tm, d), lambda i: (i, 0)),
                  pl.BlockSpec((1, d), lambda i: (0, 0))],
        out_specs=pl.BlockSpec((tm, d), lambda i: (i, 0)),
        out_shape=jax.ShapeDtypeStruct((m, d), BF16),
        compiler_params=_cparams(1),
        name="rmsnorm",
    )(x, g.reshape(1, d))


def _mm_kernel(a_ref, b_ref, o_ref):
    o_ref[...] = jnp.dot(a_ref[...], b_ref[...], preferred_element_type=F32).astype(o_ref.dtype)


def matmul(a, b, out_dtype, tm, tn, name):
    m, k = a.shape
    _, n = b.shape
    tm, tn = min(tm, m), min(tn, n)
    return pl.pallas_call(
        _mm_kernel,
        grid=(m // tm, n // tn),
        in_specs=[pl.BlockSpec((tm, k), lambda i, j: (i, 0)),
                  pl.BlockSpec((k, tn), lambda i, j: (0, j))],
        out_specs=pl.BlockSpec((tm, tn), lambda i, j: (i, j)),
        out_shape=jax.ShapeDtypeStruct((m, n), out_dtype),
        compiler_params=_cparams(2),
        name=name,
    )(a, b)


def _mm_res_kernel(a_ref, b_ref, r_ref, o_ref, acc_ref, *, nk):
    k = pl.program_id(2)

    @pl.when(k == 0)
    def _():
        acc_ref[...] = jnp.zeros_like(acc_ref)

    acc_ref[...] += jnp.dot(a_ref[...], b_ref[...], preferred_element_type=F32)

    @pl.when(k == nk - 1)
    def _():
        o_ref[...] = r_ref[...] + acc_ref[...]


def matmul_residual(a, b, res, tm, tn, tk, name):
    m, k = a.shape
    _, n = b.shape
    tm, tn, tk = min(tm, m), min(tn, n), min(tk, k)
    nk = k // tk
    return pl.pallas_call(
        functools.partial(_mm_res_kernel, nk=nk),
        grid=(m // tm, n // tn, nk),
        in_specs=[pl.BlockSpec((tm, tk), lambda i, j, kk: (i, kk)),
                  pl.BlockSpec((tk, tn), lambda i, j, kk: (kk, j)),
                  pl.BlockSpec((tm, tn), lambda i, j, kk: (i, j))],
        out_specs=pl.BlockSpec((tm, tn), lambda i, j, kk: (i, j)),
        out_shape=jax.ShapeDtypeStruct((m, n), F32),
        scratch_shapes=[pltpu.VMEM((tm, tn), F32)],
        compiler_params=_cparams(3),
        name=name,
    )(a, b, res)


def _ffn_up_kernel(h_ref, wg_ref, wu_ref, o_ref):
    h = h_ref[...]
    a = jnp.dot(h, wg_ref[...], preferred_element_type=F32)
    b = jnp.dot(h, wu_ref[...], preferred_element_type=F32)
    o_ref[...] = (a * jax.nn.sigmoid(a) * b).astype(o_ref.dtype)


def ffn_up(h, wg, wu, tm, tn):
    m, k = h.shape
    _, n = wg.shape
    tm, tn = min(tm, m), min(tn, n)
    return pl.pallas_call(
        _ffn_up_kernel,
        grid=(m // tm, n // tn),
        in_specs=[pl.BlockSpec((tm, k), lambda i, j: (i, 0)),
                  pl.BlockSpec((k, tn), lambda i, j: (0, j)),
                  pl.BlockSpec((k, tn), lambda i, j: (0, j))],
        out_specs=pl.BlockSpec((tm, tn), lambda i, j: (i, j)),
        out_shape=jax.ShapeDtypeStruct((m, n), BF16),
        compiler_params=_cparams(2),
        name="ffn_up",
    )(h, wg, wu)


def _mla_prep_kernel(hq_ref, hkv_ref, kr_ref, gqa_ref, gkva_ref, wuq_ref, wukv_ref,
                     gq_ref, gqs_ref, gk_ref, gks_ref, cos_ref, sin_ref,
                     q_ref, k_ref, v_ref):
    cos = cos_ref[...]
    sin = sin_ref[...]
    scale = QK_HEAD ** -0.5

    hq = hq_ref[...]
    cq = hq * lax.rsqrt(jnp.mean(hq * hq, axis=-1, keepdims=True) + EPS) * gqa_ref[...]
    q = jnp.dot(cq.astype(BF16), wuq_ref[...], preferred_element_type=F32)
    gq = gq_ref[...]
    gqs = gqs_ref[...]
    for h in range(MLA_HEADS):
        qh = q[:, h * QK_PAD:(h + 1) * QK_PAD]
        qs = q[:, MLA_HEADS * QK_PAD + h * LANE:MLA_HEADS * QK_PAD + (h + 1) * LANE]
        ss = jnp.sum(qh * qh, axis=-1, keepdims=True)
        rs = lax.rsqrt(ss * (1.0 / QK_HEAD) + EPS) * scale
        q_ref[h, :, 0:LANE] = (qh[:, 0:LANE] * rs * gq[:, 0:LANE]).astype(q_ref.dtype)
        rot = qh[:, LANE:QK_PAD] * gq[:, LANE:QK_PAD] * cos + qs * gqs * sin
        q_ref[h, :, LANE:QK_PAD] = (rot * rs).astype(q_ref.dtype)

    hkv = hkv_ref[...]
    ckv = hkv * lax.rsqrt(jnp.mean(hkv * hkv, axis=-1, keepdims=True) + EPS) * gkva_ref[...]
    kv = jnp.dot(ckv.astype(BF16), wukv_ref[...], preferred_element_type=F32)
    kr = kr_ref[:, 0:LANE]
    krs = kr_ref[:, LANE:2 * LANE]
    gk = gk_ref[...]
    ss_pe = jnp.sum(kr * kr, axis=-1, keepdims=True)
    krot = kr * gk[:, LANE:QK_PAD] * cos + krs * gks_ref[...] * sin
    for h in range(MLA_HEADS):
        kn = kv[:, h * 2 * LANE:h * 2 * LANE + LANE]
        ss = jnp.sum(kn * kn, axis=-1, keepdims=True) + ss_pe
        rs = lax.rsqrt(ss * (1.0 / QK_HEAD) + EPS)
        k_ref[h, :, 0:LANE] = (kn * rs * gk[:, 0:LANE]).astype(k_ref.dtype)
        k_ref[h, :, LANE:QK_PAD] = (krot * rs).astype(k_ref.dtype)
        v_ref[h] = kv[:, h * 2 * LANE + LANE:(h + 1) * 2 * LANE].astype(v_ref.dtype)


def mla_prep(proj_s, gqa, gkva, wuq_p, wukv, gq, gqs, gk, gks, cos, sin, tm=512):
    s = proj_s.shape[0]
    tm = min(tm, s)
    row = lambda c: (lambda i: (i, c))
    const = lambda i: (0, 0)
    return pl.pallas_call(
        _mla_prep_kernel,
        grid=(s // tm,),
        in_specs=[pl.BlockSpec((tm, Q_LORA), row(PS_HQ // Q_LORA)),
                  pl.BlockSpec((tm, KV_LORA), row(PS_HKV // KV_LORA)),
                  pl.BlockSpec((tm, 256), row(PS_KR // 256)),
                  pl.BlockSpec((1, Q_LORA), const),
                  pl.BlockSpec((1, KV_LORA), const),
                  pl.BlockSpec(wuq_p.shape, const),
                  pl.BlockSpec(wukv.shape, const),
                  pl.BlockSpec((1, QK_PAD), const),
                  pl.BlockSpec((1, LANE), const),
                  pl.BlockSpec((1, QK_PAD), const),
                  pl.BlockSpec((1, LANE), const),
                  pl.BlockSpec((tm, LANE), row(0)),
                  pl.BlockSpec((tm, LANE), row(0))],
        out_specs=[pl.BlockSpec((MLA_HEADS, tm, QK_PAD), lambda i: (0, i, 0)),
                   pl.BlockSpec((MLA_HEADS, tm, QK_PAD), lambda i: (0, i, 0)),
                   pl.BlockSpec((MLA_HEADS, tm, V_HEAD), lambda i: (0, i, 0))],
        out_shape=[jax.ShapeDtypeStruct((MLA_HEADS, s, QK_PAD), BF16),
                   jax.ShapeDtypeStruct((MLA_HEADS, s, QK_PAD), BF16),
                   jax.ShapeDtypeStruct((MLA_HEADS, s, V_HEAD), BF16)],
        compiler_params=_cparams(1),
        name="mla_prep",
    )(proj_s, proj_s, proj_s, gqa, gkva, wuq_p, wukv, gq, gqs, gk, gks, cos, sin)


def _flash_kernel(it_ref, jt_ref, q_ref, k_ref, v_ref, o_ref, m_ref, l_ref, acc_ref):
    p = pl.program_id(1)
    i = it_ref[p]
    j = jt_ref[p]

    @pl.when(j == 0)
    def _():
        m_ref[...] = jnp.full_like(m_ref, NEG)
        l_ref[...] = jnp.zeros_like(l_ref)
        acc_ref[...] = jnp.zeros_like(acc_ref)

    def update(mask_diag):
        s = lax.dot_general(q_ref[...], k_ref[...], (((1,), (1,)), ((), ())),
                            preferred_element_type=F32)
        if mask_diag:
            row = lax.broadcasted_iota(jnp.int32, s.shape, 0)
            col = lax.broadcasted_iota(jnp.int32, s.shape, 1)
            s = jnp.where(col <= row, s, NEG)
        m_prev = m_ref[...]
        m_new = jnp.maximum(m_prev, jnp.max(s, axis=-1, keepdims=True))
        alpha = jnp.exp(m_prev - m_new)
        pm = jnp.exp(s - m_new)
        l_ref[...] = alpha * l_ref[...] + jnp.sum(pm, axis=-1, keepdims=True)
        acc_ref[...] = alpha * acc_ref[...] + jnp.dot(pm.astype(v_ref.dtype), v_ref[...],
                                                      preferred_element_type=F32)
        m_ref[...] = m_new

    @pl.when(j < i)
    def _():
        update(False)

    @pl.when(j == i)
    def _():
        update(True)
        o_ref[...] = (acc_ref[...] / l_ref[...]).astype(o_ref.dtype)


def flash_mla(q, k, v, t=1024):
    h, s, _ = q.shape
    t = min(t, s)
    n = s // t
    pairs = [(i, j) for i in range(n) for j in range(i + 1)]
    it = jnp.asarray(np.array([p[0] for p in pairs], np.int32))
    jt = jnp.asarray(np.array([p[1] for p in pairs], np.int32))
    grid_spec = pltpu.PrefetchScalarGridSpec(
        num_scalar_prefetch=2,
        grid=(h, len(pairs)),
        in_specs=[pl.BlockSpec((None, t, QK_PAD), lambda hh, p, it, jt: (hh, it[p], 0)),
                  pl.BlockSpec((None, t, QK_PAD), lambda hh, p, it, jt: (hh, jt[p], 0)),
                  pl.BlockSpec((None, t, V_HEAD), lambda hh, p, it, jt: (hh, jt[p], 0))],
        out_specs=pl.BlockSpec((t, V_HEAD), lambda hh, p, it, jt: (it[p], hh)),
        scratch_shapes=[pltpu.VMEM((t, 1), F32), pltpu.VMEM((t, 1), F32),
                        pltpu.VMEM((t, V_HEAD), F32)],
    )
    return pl.pallas_call(
        _flash_kernel,
        grid_spec=grid_spec,
        out_shape=jax.ShapeDtypeStruct((s, h * V_HEAD), BF16),
        compiler_params=_cparams(2),
        name="flash_mla",
    )(it, jt, q, k, v)


def _norm64(x, g, scale):
    lane = lax.broadcasted_iota(jnp.int32, (x.shape[0], LANE), 1)
    lo_half = lane < DIL_QK
    outs = []
    for t in range(x.shape[1] // LANE):
        xt = x[:, t * LANE:(t + 1) * LANE]
        x2 = xt * xt
        lo = jnp.sum(jnp.where(lo_half, x2, 0.0), axis=-1, keepdims=True)
        hi = jnp.sum(jnp.where(lo_half, 0.0, x2), axis=-1, keepdims=True)
        rs = jnp.where(lo_half, lax.rsqrt(lo * (1.0 / DIL_QK) + EPS),
                       lax.rsqrt(hi * (1.0 / DIL_QK) + EPS))
        y = xt * rs * g[:, t * LANE:(t + 1) * LANE]
        outs.append(y * scale if scale != 1.0 else y)
    return outs


def _dil_prep_kernel(dq_ref, dk_ref, dv_ref, gq_ref, gk_ref, q0_ref, q1_ref, q2_ref, k_ref, v_ref):
    qn = _norm64(dq_ref[...], gq_ref[...], DIL_QK ** -0.5)
    per = DQK_W // LANE
    for g, ref in enumerate((q0_ref, q1_ref, q2_ref)):
        for t in range(per):
            ref[:, t * LANE:(t + 1) * LANE] = qn[g * per + t].astype(ref.dtype)
    kn = _norm64(dk_ref[...], gk_ref[...], 1.0)
    for t in range(per):
        k_ref[:, t * LANE:(t + 1) * LANE] = kn[t].astype(k_ref.dtype)
    v_ref[...] = dv_ref[...].astype(v_ref.dtype)


def dil_prep(proj_s, gq, gk, tm=512):
    s = proj_s.shape[0]
    tm = min(tm, s)
    row = lambda c: (lambda i: (i, c))
    const = lambda i: (0, 0)
    qk_out = jax.ShapeDtypeStruct((s, DQK_W), BF16)
    return pl.pallas_call(
        _dil_prep_kernel,
        grid=(s // tm,),
        in_specs=[pl.BlockSpec((tm, 3 * DQK_W), row(PS_DQ // (3 * DQK_W))),
                  pl.BlockSpec((tm, DQK_W), row(PS_DK // DQK_W)),
                  pl.BlockSpec((tm, DIL_OUT), row(PS_DV // DIL_OUT)),
                  pl.BlockSpec((1, 3 * DQK_W), const),
                  pl.BlockSpec((1, DQK_W), const)],
        out_specs=[pl.BlockSpec((tm, DQK_W), row(0))] * 4 + [pl.BlockSpec((tm, DIL_OUT), row(0))],
        out_shape=[qk_out] * 4 + [jax.ShapeDtypeStruct((s, DIL_OUT), BF16)],
        compiler_params=_cparams(1),
        name="dil_prep",
    )(proj_s, proj_s, proj_s, gq, gk)


def _bias_kernel(tab_ref, bkt_ref, o_ref):
    gj = pl.program_id(0)
    bkt = bkt_ref[...]
    acc = jnp.full(bkt.shape, NEG, F32)
    for b in range(N_BUCKETS):
        acc = jnp.where(bkt == b, tab_ref[b, gj], acc)
    o_ref[...] = acc


def _bucket_maps():
    qi = np.arange(DIL_BLK)[:, None]
    ki = np.arange(2 * DIL_BLK)[None, :]
    dist = DIL_BLK + qi - ki
    maps = []
    for window, r in DIL_CONFIGS:
        span = window // r
        band = (dist >= 0) & (dist <= span)
        d_tok = np.clip(dist, 0, span) * r
        max_exact = N_BUCKETS // 2
        d = np.maximum(d_tok, 1).astype(np.float32)
        large = max_exact + (np.log(d / np.float32(max_exact)) / np.float32(math.log(MAX_DISTANCE / max_exact))
                             * np.float32(N_BUCKETS - max_exact)).astype(np.int32)
        large = np.minimum(large, N_BUCKETS - 1)
        bucket = np.where(d_tok < max_exact, d_tok, large)
        maps.append(np.where(band, bucket, -1).astype(np.int32))
    return np.stack(maps)


def dil_bias_table(rel_bias):
    bkt = jnp.asarray(_bucket_maps())
    n = DIL_GROUPS * DIL_SLOTS
    return pl.pallas_call(
        _bias_kernel,
        grid=(n,),
        in_specs=[pl.BlockSpec(memory_space=pltpu.SMEM),
                  pl.BlockSpec((None, DIL_BLK, 2 * DIL_BLK), lambda gj: (gj // DIL_SLOTS, 0, 0))],
        out_specs=pl.BlockSpec((None, DIL_BLK, 2 * DIL_BLK), lambda gj: (gj, 0, 0)),
        out_shape=jax.ShapeDtypeStruct((n, DIL_BLK, 2 * DIL_BLK), F32),
        compiler_params=_cparams(1),
        name="dil_bias",
    )(rel_bias, bkt)


def _dil_attn_kernel(q_ref, kp_ref, kc_ref, vp_ref, vc_ref, bias_ref, o_ref, ml_ref,
                     kbuf, vbuf, *, nbs):
    first = pl.program_id(1) == 0
    kbuf[0:DIL_BLK] = kp_ref[...]
    kbuf[DIL_BLK:] = kc_ref[...]
    vbuf[0:DIL_BLK] = vp_ref[...]
    vbuf[DIL_BLK:] = vc_ref[...]
    lane = lax.broadcasted_iota(jnp.int32, (DIL_BLK, LANE), 1)
    lo_half = lane < DIL_QK
    col = lax.broadcasted_iota(jnp.int32, (DIL_BLK, 2 * DIL_BLK), 1)

    def block(b, carry):
        r0 = pl.multiple_of(b * DIL_BLK, DIL_BLK)
        no_prev = jnp.logical_and(first, b == 0)
        kill = jnp.where(jnp.logical_and(no_prev, col < DIL_BLK), NEG, 0.0)
        ml = jnp.zeros((DIL_BLK, LANE), F32)
        for j in range(DIL_SLOTS):
            t = j // 2
            qt = q_ref[pl.ds(r0, DIL_BLK), t * LANE:(t + 1) * LANE]
            zero = jnp.zeros_like(qt)
            qj = jnp.where(lo_half, qt, zero) if j % 2 == 0 else jnp.where(lo_half, zero, qt)
            kt = kbuf[pl.ds(r0, 2 * DIL_BLK), t * LANE:(t + 1) * LANE]
            s = lax.dot_general(qj, kt, (((1,), (1,)), ((), ())), preferred_element_type=F32)
            logits = s + bias_ref[j] + kill
            m = jnp.max(logits, axis=-1, keepdims=True)
            p = jnp.exp(logits - m)
            l = jnp.sum(p, axis=-1, keepdims=True)
            vt = vbuf[pl.ds(r0, 2 * DIL_BLK), j * DIL_V:(j + 1) * DIL_V]
            o = jnp.dot(p.astype(vt.dtype), vt, preferred_element_type=F32)
            o_ref[pl.ds(r0, DIL_BLK), j * DIL_V:(j + 1) * DIL_V] = o
            ml = jnp.where(lane == j, m, ml)
            ml = jnp.where(lane == 8 + j, l, ml)
        ml_ref[pl.ds(r0, DIL_BLK), :] = ml
        return carry

    lax.fori_loop(0, nbs, block, 0)


def dil_attn(q, k, v, bias_g, r, nbs_max=8):
    s = q.shape[0]
    l = s // r
    nb = l // DIL_BLK
    nbs = min(nbs_max, nb)
    rows = nbs * DIL_BLK
    q2, k2, v2 = q.reshape(l, r * DQK_W), k.reshape(l, r * DQK_W), v.reshape(l, r * DIL_OUT)
    cur = lambda rho, i: (i, rho)
    prev = lambda rho, i: (jnp.maximum(i * nbs - 1, 0), rho)
    o, ml = pl.pallas_call(
        functools.partial(_dil_attn_kernel, nbs=nbs),
        grid=(r, nb // nbs),
        in_specs=[pl.BlockSpec((rows, DQK_W), cur),
                  pl.BlockSpec((DIL_BLK, DQK_W), prev),
                  pl.BlockSpec((rows, DQK_W), cur),
                  pl.BlockSpec((DIL_BLK, DIL_OUT), prev),
                  pl.BlockSpec((rows, DIL_OUT), cur),
                  pl.BlockSpec((DIL_SLOTS, DIL_BLK, 2 * DIL_BLK), lambda rho, i: (0, 0, 0))],
        out_specs=[pl.BlockSpec((rows, DIL_OUT), cur),
                   pl.BlockSpec((rows, LANE), cur)],
        out_shape=[jax.ShapeDtypeStruct((l, r * DIL_OUT), F32),
                   jax.ShapeDtypeStruct((l, r * LANE), F32)],
        scratch_shapes=[pltpu.VMEM((rows + DIL_BLK, DQK_W), BF16),
                        pltpu.VMEM((rows + DIL_BLK, DIL_OUT), BF16)],
        compiler_params=_cparams(2),
        name=f"dil_attn_r{r}",
    )(q2, k2, k2, v2, v2, bias_g)
    return o.reshape(s, DIL_OUT), ml.reshape(s, LANE)


def _dil_merge_kernel(o0_ref, o1_ref, o2_ref, ml0_ref, ml1_ref, ml2_ref, y_ref):
    mls = (ml0_ref[...], ml1_ref[...], ml2_ref[...])
    os_ = (o0_ref, o1_ref, o2_ref)
    for j in range(DIL_SLOTS):
        ms = [ml[:, j:j + 1] for ml in mls]
        ls = [ml[:, 8 + j:9 + j] for ml in mls]
        m_all = jnp.maximum(jnp.maximum(ms[0], ms[1]), ms[2])
        ws = [jnp.exp(m - m_all) for m in ms]
        den = ws[0] * ls[0] + ws[1] * ls[1] + ws[2] * ls[2]
        sl = slice(j * DIL_V, (j + 1) * DIL_V)
        num = ws[0] * os_[0][:, sl] + ws[1] * os_[1][:, sl] + ws[2] * os_[2][:, sl]
        y_ref[:, sl] = (num / den).astype(y_ref.dtype)


def dil_merge(os_, mls, tm=512):
    s = os_[0].shape[0]
    tm = min(tm, s)
    row = lambda i: (i, 0)
    return pl.pallas_call(
        _dil_merge_kernel,
        grid=(s // tm,),
        in_specs=[pl.BlockSpec((tm, DIL_OUT), row)] * 3 + [pl.BlockSpec((tm, LANE), row)] * 3,
        out_specs=pl.BlockSpec((tm, DIL_OUT), row),
        out_shape=jax.ShapeDtypeStruct((s, DIL_OUT), BF16),
        compiler_params=_cparams(1),
        name="dil_merge",
    )(*os_, *mls)


POOL_HALO = 16


def _pool_kernel(p_ref, halo_ref, w_ref, sc_ref, y_ref, buf, *, tm):
    i = pl.program_id(0)
    halo = halo_ref[...]
    buf[0:POOL_HALO] = jnp.where(i == 0, jnp.zeros_like(halo), halo)
    buf[POOL_HALO:] = p_ref[...]
    t = i * tm + lax.broadcasted_iota(jnp.int32, (tm, 1), 0)
    for g, w in enumerate(POOL_WINDOWS):
        sl = slice(g * POOL_GROUP_CH, (g + 1) * POOL_GROUP_CH)
        cur = buf[POOL_HALO:POOL_HALO + tm, sl]
        wsum = cur
        for kk in range(1, w):
            wsum = wsum + buf[POOL_HALO - kk:POOL_HALO - kk + tm, sl]
        cnt = jnp.minimum(t + 1, w).astype(F32)
        pooled = wsum / cnt - cur
        y = jnp.dot(pooled.astype(BF16), w_ref[g], preferred_element_type=F32) * sc_ref[:, sl]
        y_ref[:, sl] = y.astype(y_ref.dtype)


def pool_branch(proj_s, pool_w, pool_scale, tm=512):
    s = proj_s.shape[0]
    tm = min(tm, s)
    per = tm // POOL_HALO
    return pl.pallas_call(
        functools.partial(_pool_kernel, tm=tm),
        grid=(s // tm,),
        in_specs=[pl.BlockSpec((tm, POOL_CH), lambda i: (i, PS_POOL // POOL_CH)),
                  pl.BlockSpec((POOL_HALO, POOL_CH),
                               lambda i: (jnp.maximum(i * per - 1, 0), PS_POOL // POOL_CH)),
                  pl.BlockSpec(pool_w.shape, lambda i: (0, 0, 0)),
                  pl.BlockSpec((1, POOL_CH), lambda i: (0, 0))],
        out_specs=pl.BlockSpec((tm, POOL_CH), lambda i: (i, 0)),
        out_shape=jax.ShapeDtypeStruct((s, POOL_CH), BF16),
        scratch_shapes=[pltpu.VMEM((tm + POOL_HALO, POOL_CH), F32)],
        compiler_params=_cparams(1),
        name="pool",
    )(proj_s, proj_s, pool_w, pool_scale)


def _branch_kernel(ym_ref, yd_ref, yp_ref, wb_ref, g0_ref, g1_ref, g2_ref, o_ref):
    zm = jnp.dot(ym_ref[...], wb_ref[0:MLA_OUT], preferred_element_type=F32)
    zd = jnp.dot(yd_ref[...], wb_ref[MLA_OUT:MLA_OUT + DIL_OUT], preferred_element_type=F32)
    zp = jnp.dot(yp_ref[...], wb_ref[MLA_OUT + DIL_OUT:MIX_W], preferred_element_type=F32)
    sig = lambda ref: jax.nn.sigmoid(ref[...].astype(F32))
    o_ref[...] = (sig(g0_ref) * zm + sig(g1_ref) * zd + sig(g2_ref) * zp).astype(o_ref.dtype)


def branch_merge(ym, yd, yp, wb, gates, tm=1024, tn=1024):
    s = ym.shape[0]
    tm, tn = min(tm, s), min(tn, D_MODEL)
    nj = D_MODEL // tn
    row = lambda i, j: (i, 0)
    return pl.pallas_call(
        _branch_kernel,
        grid=(s // tm, nj),
        in_specs=[pl.BlockSpec((tm, MLA_OUT), row),
                  pl.BlockSpec((tm, DIL_OUT), row),
                  pl.BlockSpec((tm, POOL_CH), row),
                  pl.BlockSpec((MIX_W, tn), lambda i, j: (0, j)),
                  pl.BlockSpec((tm, tn), lambda i, j: (i, j)),
                  pl.BlockSpec((tm, tn), lambda i, j: (i, nj + j)),
                  pl.BlockSpec((tm, tn), lambda i, j: (i, 2 * nj + j))],
        out_specs=pl.BlockSpec((tm, tn), lambda i, j: (i, j)),
        out_shape=jax.ShapeDtypeStruct((s, D_MODEL), BF16),
        compiler_params=_cparams(2),
        name="branch_merge",
    )(ym, yd, yp, wb, gates, gates, gates)


def _prep_weights(w_in, w_uq, w_ukv, mla_q_norm, mla_k_norm, dil_q_norm, dil_k_norm,
                  w_gate_up, w_down):
    depth = w_in.shape[0]
    half = QK_ROPE // 2
    z = lambda n, like: jnp.zeros(like.shape[:-1] + (n,), like.dtype)

    kr = w_in[:, :, O_HKR:O_HKR + QK_ROPE]
    kr_sw = jnp.concatenate([kr[:, :, half:], kr[:, :, :half]], axis=-1)
    w_s = jnp.concatenate([
        w_in[:, :, O_DQ:O_DQ + 3 * DQK_W], w_in[:, :, O_DK:O_DK + DQK_W],
        w_in[:, :, O_DV:O_DV + DIL_OUT], w_in[:, :, O_HQ:O_HQ + Q_LORA],
        w_in[:, :, O_HKV:O_HKV + KV_LORA], w_in[:, :, O_POOL:O_POOL + POOL_CH],
        kr, z(LANE - QK_ROPE, kr), kr_sw, z(LANE - QK_ROPE, kr), z(PS_W - PS_USED, kr)],
        axis=-1).astype(BF16)
    w_g = w_in[:, :, O_GATE:].astype(BF16)

    heads, swaps = [], []
    for h in range(MLA_HEADS):
        b = h * QK_HEAD
        heads += [w_uq[:, :, b:b + QK_HEAD], z(QK_PAD - QK_HEAD, w_uq)]
        swaps += [w_uq[:, :, b + QK_NOPE + half:b + QK_HEAD], w_uq[:, :, b + QK_NOPE:b + QK_NOPE + half],
                  z(LANE - QK_ROPE, w_uq)]
    wuq_p = jnp.concatenate(heads + swaps, axis=-1).astype(BF16)
    wukv = w_ukv.astype(BF16)

    def gains(g):
        full = jnp.concatenate([g, z(QK_PAD - QK_HEAD, g)], axis=-1)
        sw = jnp.concatenate([g[:, QK_NOPE + half:], g[:, QK_NOPE:QK_NOPE + half],
                              z(LANE - QK_ROPE, g)], axis=-1)
        return full.reshape(depth, 1, QK_PAD), sw.reshape(depth, 1, LANE)

    gq, gqs = gains(mla_q_norm)
    gk, gks = gains(mla_k_norm)
    dgq = jnp.tile(dil_q_norm, (1, DIL_GROUPS * DIL_SLOTS)).reshape(depth, 1, 3 * DQK_W)
    dgk = jnp.tile(dil_k_norm, (1, DIL_SLOTS)).reshape(depth, 1, DQK_W)

    pad_ff = lambda w: jnp.concatenate([w, z(D_FFP - D_FF, w)], axis=-1)
    wg = pad_ff(w_gate_up[:, :, :D_FF]).astype(BF16)
    wu = pad_ff(w_gate_up[:, :, D_FF:]).astype(BF16)
    wd = jnp.concatenate([w_down, jnp.zeros((depth, D_FFP - D_FF, D_MODEL), w_down.dtype)],
                         axis=1).astype(BF16)
    return w_s, w_g, wuq_p, wukv, gq, gqs, gk, gks, dgq, dgk, wg, wu, wd


def _rope_tables(s):
    half = QK_ROPE // 2
    inv = ROPE_THETA ** (-jnp.arange(half, dtype=F32) / half)
    ang = jnp.arange(s).astype(F32)[:, None] * inv[None, :]
    cos, sin = jnp.cos(ang), jnp.sin(ang)
    pad = jnp.zeros((s, LANE - QK_ROPE), F32)
    return (jnp.concatenate([cos, cos, pad], axis=-1), jnp.concatenate([-sin, sin, pad], axis=-1))


def kernel(x, w_in, attn_norm, q_a_norm, kv_a_norm, w_uq, w_ukv, mla_q_norm, mla_k_norm,
           dil_q_norm, dil_k_norm, rel_bias, pool_w, pool_scale, w_branch, w_out,
           ffn_norm, w_gate_up, w_down):
    b, s, d = x.shape
    depth = w_in.shape[0]
    (w_s, w_g, wuq_p, wukv, gq, gqs, gk, gks, dgq, dgk, wg, wu, wd) = _prep_weights(
        w_in, w_uq, w_ukv, mla_q_norm, mla_k_norm, dil_q_norm, dil_k_norm, w_gate_up, w_down)
    wb = w_branch.astype(BF16)
    wo = w_out.astype(BF16)
    pw = pool_w.astype(BF16)
    cos, sin = _rope_tables(s)
    bias = dil_bias_table(rel_bias)

    outs = []
    for bi in range(b):
        xs = x[bi]
        for l in range(depth):
            h = rmsnorm_bf16(xs, attn_norm[l])
            proj_s = matmul(h, w_s[l], F32, tm=1024, tn=512, name="in_proj_small")
            gates = matmul(h, w_g[l], BF16, tm=1024, tn=1024, name="in_proj_gates")

            q, k, v = mla_prep(proj_s, q_a_norm[l].reshape(1, -1), kv_a_norm[l].reshape(1, -1),
                               wuq_p[l], wukv[l], gq[l], gqs[l], gk[l], gks[l], cos, sin)
            y_mla = flash_mla(q, k, v)

            dq0, dq1, dq2, dk, dv = dil_prep(proj_s, dgq[l], dgk[l])
            os_, mls = [], []
            for g, (dqg, (_, r)) in enumerate(zip((dq0, dq1, dq2), DIL_CONFIGS)):
                o, ml = dil_attn(dqg, dk, dv, bias[g * DIL_SLOTS:(g + 1) * DIL_SLOTS], r)
                os_.append(o)
                mls.append(ml)
            y_dil = dil_merge(os_, mls)

            y_pool = pool_branch(proj_s, pw[l], pool_scale[l].reshape(1, -1))

            merged = branch_merge(y_mla, y_dil, y_pool, wb[l], gates)
            xs = matmul_residual(merged, wo[l], xs, tm=1024, tn=1024, tk=D_MODEL, name="out_proj")

            h2 = rmsnorm_bf16(xs, ffn_norm[l])
            hm = ffn_up(h2, wg[l], wu[l], tm=1024, tn=512)
            xs = matmul_residual(hm, wd[l], xs, tm=1024, tn=1024, tk=2816, name="ffn_down")
        outs.append(xs)
    return jnp.stack(outs)
```
